```python
import math
import jax, jax.numpy as jnp
from jax import lax
import numpy as np

D_MODEL = 4096
BATCH = 2
SEQ = 4096
DEPTH = 1
DEC_BATCH = 32
DEC_SEQ = 8
PAST_LEN = 8192
PAGE_SIZE = 128

SB_HEADS = 16
SB_HEAD_DIM = 128
SB_WIDTH = SB_HEADS * SB_HEAD_DIM
SB_SCALE = SB_HEAD_DIM ** -0.5
SB_BIAS_HI = -3.0
SB_BIAS_LO = -8.0
Q_BLOCK = 128
ML_HEADS = 8
ML_QK_DIM = 128
ML_V_DIM = 256
ML_QK_WIDTH = ML_HEADS * ML_QK_DIM
ML_V_WIDTH = ML_HEADS * ML_V_DIM
ML_CHUNK = 128
FFN_HIDDEN = ((-(-8 * D_MODEL // 3) + 255) // 256) * 256
PLE_DIM = 256
LN_EPS = 1e-5
DEEPNORM_ALPHA = (2 * DEPTH) ** 0.25
DEEPNORM_BETA = (8 * DEPTH) ** -0.25

kernel_name = 'stick_breaking_mlstm_hybrid_step'


def branch_split_sizes():
    return (SB_WIDTH, SB_WIDTH, SB_WIDTH,
            ML_QK_WIDTH, ML_QK_WIDTH, ML_V_WIDTH,
            ML_V_WIDTH, ML_HEADS, ML_HEADS,
            D_MODEL, D_MODEL)


def split_points():
    return [int(c) for c in np.cumsum(branch_split_sizes())[:-1]]


def layer_norm(x, g, b):
    xf = x.astype(jnp.float32)
    mu = xf.mean(-1, keepdims=True)
    var = jnp.square(xf - mu).mean(-1, keepdims=True)
    return ((xf - mu) * lax.rsqrt(var + LN_EPS)).astype(x.dtype) * g + b


def head_norm(h, g):
    B, T = h.shape[:2]
    mu = h.mean(-1, keepdims=True)
    var = jnp.square(h - mu).mean(-1, keepdims=True)
    return ((h - mu) * lax.rsqrt(var + LN_EPS)).reshape(B, T, -1) * g.astype(jnp.float32)


def branch_inputs(x, w_in, b_igate, b_fgate):
    B, T, _ = x.shape
    proj = jnp.einsum('btd,de->bte', x, w_in)
    sb_q, sb_k, sb_v, ml_q, ml_k, ml_v, ml_o, ml_i, ml_f, g_sb, g_ml = jnp.split(proj, split_points(), axis=-1)
    sb = (sb_q.reshape(B, T, SB_HEADS, SB_HEAD_DIM),
          sb_k.reshape(B, T, SB_HEADS, SB_HEAD_DIM),
          sb_v.reshape(B, T, SB_HEADS, SB_HEAD_DIM))
    i_pre = (ml_i + b_igate).astype(jnp.float32).transpose(0, 2, 1)
    log_f = jax.nn.log_sigmoid((ml_f + b_fgate).astype(jnp.float32)).transpose(0, 2, 1)
    ml = (ml_q.reshape(B, T, ML_HEADS, ML_QK_DIM).astype(jnp.float32),
          ml_k.reshape(B, T, ML_HEADS, ML_QK_DIM).astype(jnp.float32) * (ML_QK_DIM ** -0.5),
          ml_v.reshape(B, T, ML_HEADS, ML_V_DIM).astype(jnp.float32),
          i_pre, log_f)
    return sb, ml, ml_o, g_sb, g_ml


def sb_scores(q, k, b_sb):
    z = jnp.einsum('bqhd,bkhd->bhqk', q, k).astype(jnp.float32) * SB_SCALE
    return z + b_sb.astype(jnp.float32)[None, :, None, None]


def sb_weights(z, mask):
    log_1m_beta = jnp.where(mask, jax.nn.log_sigmoid(-z), 0.0)
    later = lax.cumsum(log_1m_beta, axis=3, reverse=True) - log_1m_beta
    return jnp.where(mask, jnp.exp(jax.nn.log_sigmoid(z) + later), 0.0)


def sb_prompt(q, k, v, b_sb):
    B, T = q.shape[:2]
    nb = T // Q_BLOCK
    k_pos = jnp.arange(T)
    qb = q.reshape(B, nb, Q_BLOCK, SB_HEADS, SB_HEAD_DIM).transpose(1, 0, 2, 3, 4)

    def one_block(args):
        q_blk, blk = args
        q_pos = blk * Q_BLOCK + jnp.arange(Q_BLOCK)
        a = sb_weights(sb_scores(q_blk, k, b_sb), k_pos[None, :] < q_pos[:, None])
        return jnp.einsum('bhqk,bkhd->bqhd', a.astype(v.dtype), v)

    out = lax.map(one_block, (qb, jnp.arange(nb)))
    return out.transpose(1, 0, 2, 3, 4).reshape(B, T, SB_HEADS, SB_HEAD_DIM)


def sb_sample(q, k, v, cache_k, cache_v, page_table, b_sb):
    B, T = q.shape[:2]
    past = page_table.shape[1] * PAGE_SIZE
    k_past = cache_k[page_table].reshape(B, past, SB_HEADS, SB_HEAD_DIM)
    v_past = cache_v[page_table].reshape(B, past, SB_HEADS, SB_HEAD_DIM)
    z = jnp.concatenate([sb_scores(q, k_past, b_sb), sb_scores(q, k, b_sb)], axis=-1)
    q_pos = past + jnp.arange(T)
    k_pos = jnp.arange(past + T)
    a = sb_weights(z, k_pos[None, :] < q_pos[:, None]).astype(v.dtype)
    return (jnp.einsum('bhqk,bkhd->bqhd', a[..., :past], v_past)
            + jnp.einsum('bhqk,bkhd->bqhd', a[..., past:], v))


def mlstm_chunk(state, chunk):
    C0, n0, m0 = state
    q, k, v, i_pre, log_f = chunk
    L = q.shape[1]
    F = jnp.cumsum(log_f, axis=-1)
    m = F + jnp.maximum(m0[..., None], lax.cummax(i_pre - F, axis=2))
    causal = jnp.tril(jnp.ones((L, L), dtype=bool))
    log_d = F[..., :, None] - F[..., None, :] + i_pre[..., None, :] - m[..., :, None]
    d = jnp.exp(jnp.where(causal, log_d, -jnp.inf))
    w = d * jnp.einsum('bthd,bshd->bhts', q, k)
    decay0 = jnp.exp(F + m0[..., None] - m)
    num = (jnp.einsum('bhts,bshv->bthv', w, v)
           + jnp.einsum('bthd,bhdv->bthv', q, C0) * decay0.transpose(0, 2, 1)[..., None])
    den = w.sum(-1) + jnp.einsum('bthd,bhd->bht', q, n0) * decay0
    h = num / jnp.maximum(jnp.abs(den), jnp.exp(-m)).transpose(0, 2, 1)[..., None]
    w_end = jnp.exp(F[..., -1:] - F + i_pre - m[..., -1:])
    decay_end = jnp.exp(F[..., -1] + m0 - m[..., -1])
    C1 = decay_end[..., None, None] * C0 + jnp.einsum('bhs,bshd,bshv->bhdv', w_end, k, v)
    n1 = decay_end[..., None] * n0 + jnp.einsum('bhs,bshd->bhd', w_end, k)
    return (C1, n1, m[..., -1]), h


def mlstm_prompt(q, k, v, i_pre, log_f):
    B, T, H, _ = q.shape
    nc = T // ML_CHUNK
    to_chunks = lambda t: t.reshape(B, nc, ML_CHUNK, H, -1).transpose(1, 0, 2, 3, 4)
    gate_chunks = lambda g: g.reshape(B, H, nc, ML_CHUNK).transpose(2, 0, 1, 3)
    state0 = (jnp.zeros((B, H, ML_QK_DIM, ML_V_DIM), jnp.float32),
              jnp.zeros((B, H, ML_QK_DIM), jnp.float32),
              jnp.zeros((B, H), jnp.float32))
    final, h = lax.scan(mlstm_chunk, state0,
                        (to_chunks(q), to_chunks(k), to_chunks(v), gate_chunks(i_pre), gate_chunks(log_f)))
    h = h.transpose(1, 0, 2, 3, 4).reshape(B, T, H, ML_V_DIM)
    return h, final


def merge_and_ffn(x, o_sb, h_ml, ml_o, g_sb, g_ml, p, ml_norm_g, w_sb_proj, w_ml_proj, w_out,
                  ln1_g, ln1_b, w_ffn_gate, w_ffn_up, w_ffn_down, ln2_g, ln2_b, w_ple, w_ple_gate):
    B, T, _ = x.shape
    y_sb = o_sb.reshape(B, T, SB_WIDTH) @ w_sb_proj
    y_ml = (head_norm(h_ml, ml_norm_g).astype(x.dtype) * jax.nn.sigmoid(ml_o)) @ w_ml_proj
    mixed = (jax.nn.sigmoid(g_sb) * y_sb + jax.nn.sigmoid(g_ml) * y_ml) @ w_out
    x1 = layer_norm(DEEPNORM_ALPHA * x + mixed, ln1_g, ln1_b)
    ff = (jax.nn.silu(x1 @ w_ffn_gate) * (x1 @ w_ffn_up)) @ w_ffn_down
    x2 = layer_norm(DEEPNORM_ALPHA * x1 + ff, ln2_g, ln2_b)
    return x2 + (p @ w_ple) * jax.nn.sigmoid(x2 @ w_ple_gate)


def setup_inputs(seed: int = 0) -> dict:
    key = jax.random.key(seed)
    ks = jax.random.split(key, 32)
    n_pages = PAST_LEN // PAGE_SIZE
    n_used = DEC_BATCH * n_pages
    n_phys = n_used + n_used // 4
    in_width = sum(branch_split_sizes())

    def nrm(k, shape, scale):
        return jax.random.normal(k, shape, jnp.float32) * scale

    page_table = jax.random.permutation(ks[9], n_phys)[:n_used].reshape(DEC_BATCH, n_pages).astype(jnp.int32)
    return {
        'x_prompt': nrm(ks[0], (BATCH, SEQ, D_MODEL), 1.0),
        'x_sample': nrm(ks[1], (DEC_BATCH, DEC_SEQ, D_MODEL), 1.0),
        'cache_k': nrm(ks[2], (DEPTH, n_phys, PAGE_SIZE, SB_HEADS, SB_HEAD_DIM), 1.0),
        'cache_v': nrm(ks[3], (DEPTH, n_phys, PAGE_SIZE, SB_HEADS, SB_HEAD_DIM), 1.0),
        'state_C': nrm(ks[4], (DEPTH, DEC_BATCH, ML_HEADS, ML_QK_DIM, ML_V_DIM), 1.0),
        'state_n': nrm(ks[5], (DEPTH, DEC_BATCH, ML_HEADS, ML_QK_DIM), 1.0),
        'state_m': nrm(ks[6], (DEPTH, DEC_BATCH, ML_HEADS), 1.0),
        'page_table': page_table,
        'p_prompt': nrm(ks[7], (DEPTH, BATCH, SEQ, PLE_DIM), 1.0),
        'p_sample': nrm(ks[8], (DEPTH, DEC_BATCH, DEC_SEQ, PLE_DIM), 1.0),
        'w_in': nrm(ks[10], (DEPTH, D_MODEL, in_width), D_MODEL ** -0.5),
        'b_igate': nrm(ks[11], (DEPTH, ML_HEADS), 0.1),
        'b_fgate': jnp.linspace(3.0, 6.0, ML_HEADS, dtype=jnp.float32)[None, :] + nrm(ks[12], (DEPTH, ML_HEADS), 0.1),
        'b_sb': jnp.linspace(SB_BIAS_HI, SB_BIAS_LO, SB_HEADS, dtype=jnp.float32)[None, :] + nrm(ks[26], (DEPTH, SB_HEADS), 0.1),
        'ml_norm_g': 1.0 + nrm(ks[13], (DEPTH, ML_V_WIDTH), 0.02),
        'w_sb_proj': nrm(ks[14], (DEPTH, SB_WIDTH, D_MODEL), SB_WIDTH ** -0.5),
        'w_ml_proj': nrm(ks[15], (DEPTH, ML_V_WIDTH, D_MODEL), ML_V_WIDTH ** -0.5),
        'w_out': nrm(ks[16], (DEPTH, D_MODEL, D_MODEL), D_MODEL ** -0.5 * DEEPNORM_BETA),
        'ln1_g': 1.0 + nrm(ks[17], (DEPTH, D_MODEL), 0.02),
        'ln1_b': nrm(ks[18], (DEPTH, D_MODEL), 0.02),
        'w_ffn_gate': nrm(ks[19], (DEPTH, D_MODEL, FFN_HIDDEN), D_MODEL ** -0.5),
        'w_ffn_up': nrm(ks[20], (DEPTH, D_MODEL, FFN_HIDDEN), D_MODEL ** -0.5),
        'w_ffn_down': nrm(ks[21], (DEPTH, FFN_HIDDEN, D_MODEL), FFN_HIDDEN ** -0.5 * DEEPNORM_BETA),
        'ln2_g': 1.0 + nrm(ks[22], (DEPTH, D_MODEL), 0.02),
        'ln2_b': nrm(ks[23], (DEPTH, D_MODEL), 0.02),
        'w_ple': nrm(ks[24], (DEPTH, PLE_DIM, D_MODEL), PLE_DIM ** -0.5),
        'w_ple_gate': nrm(ks[25], (DEPTH, D_MODEL, D_MODEL), D_MODEL ** -0.5),
    }


def reference(x_prompt, x_sample, cache_k, cache_v, state_C, state_n, state_m, page_table,
              p_prompt, p_sample, w_in, b_igate, b_fgate, b_sb, ml_norm_g, w_sb_proj, w_ml_proj,
              w_out, ln1_g, ln1_b, w_ffn_gate, w_ffn_up, w_ffn_down, ln2_g, ln2_b,
              w_ple, w_ple_gate):
    sdt = state_C.dtype
    hp, hs = x_prompt, x_sample
    kp_l, vp_l, Cp_l, np_l, mp_l = [], [], [], [], []
    ks_l, vs_l, Cs_l, ns_l, ms_l = [], [], [], [], []
    for l in range(DEPTH):
        out_w = (ml_norm_g[l], w_sb_proj[l], w_ml_proj[l], w_out[l], ln1_g[l], ln1_b[l],
                 w_ffn_gate[l], w_ffn_up[l], w_ffn_down[l], ln2_g[l], ln2_b[l], w_ple[l], w_ple_gate[l])

        (q, k, v), ml, ml_o, g_sb, g_ml = branch_inputs(hp, w_in[l], b_igate[l], b_fgate[l])
        o_sb = sb_prompt(q, k, v, b_sb[l])
        h_ml, (C1, n1, m1) = mlstm_prompt(*ml)
        hp = merge_and_ffn(hp, o_sb, h_ml, ml_o, g_sb, g_ml, p_prompt[l], *out_w)
        kp_l.append(k); vp_l.append(v)
        Cp_l.append(C1.astype(sdt)); np_l.append(n1.astype(sdt)); mp_l.append(m1.astype(sdt))

        (q, k, v), ml, ml_o, g_sb, g_ml = branch_inputs(hs, w_in[l], b_igate[l], b_fgate[l])
        o_sb = sb_sample(q, k, v, cache_k[l], cache_v[l], page_table, b_sb[l])
        state = (state_C[l].astype(jnp.float32), state_n[l].astype(jnp.float32), state_m[l].astype(jnp.float32))
        (C1, n1, m1), h_ml = mlstm_chunk(state, ml)
        hs = merge_and_ffn(hs, o_sb, h_ml, ml_o, g_sb, g_ml, p_sample[l], *out_w)
        ks_l.append(k); vs_l.append(v)
        Cs_l.append(C1.astype(sdt)); ns_l.append(n1.astype(sdt)); ms_l.append(m1.astype(sdt))

    y_prompt, y_sample = hp, hs
    k_prompt, v_prompt = jnp.stack(kp_l), jnp.stack(vp_l)
    C_prompt, n_prompt, m_prompt = jnp.stack(Cp_l), jnp.stack(np_l), jnp.stack(mp_l)
    k_sample, v_sample = jnp.stack(ks_l), jnp.stack(vs_l)
    C_sample, n_sample, m_sample = jnp.stack(Cs_l), jnp.stack(ns_l), jnp.stack(ms_l)
    return (y_prompt, y_sample, k_prompt, v_prompt, C_prompt, n_prompt, m_prompt,
            k_sample, v_sample, C_sample, n_sample, m_sample)
```

```python
import functools
import math

import jax
import jax.numpy as jnp
from jax import lax
from jax.experimental import pallas as pl
from jax.experimental.pallas import tpu as pltpu

F32 = jnp.float32
BF16 = jnp.bfloat16

D_MODEL = 4096
SB_HEADS = 16
SB_HEAD_DIM = 128
SB_WIDTH = SB_HEADS * SB_HEAD_DIM
SB_SCALE = SB_HEAD_DIM ** -0.5
PAGE_SIZE = 128
ML_HEADS = 8
ML_QK_DIM = 128
ML_V_DIM = 256
ML_QK_WIDTH = ML_HEADS * ML_QK_DIM
ML_V_WIDTH = ML_HEADS * ML_V_DIM
ML_CHUNK = 128
FFN_HIDDEN = 11008
FFN_HIDDEN_PAD = 11264
LN_EPS = 1e-5
DEPTH = 1
DEEPNORM_ALPHA = (2 * DEPTH) ** 0.25

LANES = 128
V7X_VMEM_BYTES = 64 * 1024 * 1024
VMEM_LIMIT_CAP = V7X_VMEM_BYTES - 8 * 1024 * 1024
PAD_GATE_NEG = -1e30
PAD_GATE_POS = 1e30


def _vmem_limit(block_bytes):
    need = 2 * block_bytes + 8 * 1024 * 1024
    return int(min(max(need, 16 * 1024 * 1024), VMEM_LIMIT_CAP))


def _nbytes(shape, dtype):
    return math.prod(shape) * jnp.dtype(dtype).itemsize


def _softplus_parts(z):
    lp = jnp.log1p(jnp.exp(-jnp.abs(z)))
    return -(jnp.maximum(z, 0.0) + lp), jnp.minimum(z, 0.0) - lp


def _log_sigmoid(z):
    return jnp.minimum(z, 0.0) - jnp.log1p(jnp.exp(-jnp.abs(z)))


def _split_bf16(x):
    hi = x.astype(BF16)
    lo = (x - hi.astype(F32)).astype(BF16)
    return hi, lo


def _mm_kernel(*refs, n_pairs, n_extras, epilogue):
    pair_refs = refs[:2 * n_pairs]
    extra_refs = refs[2 * n_pairs:2 * n_pairs + n_extras]
    out_refs = refs[2 * n_pairs + n_extras:]
    accs = [jnp.dot(pair_refs[2 * p][...], pair_refs[2 * p + 1][...], preferred_element_type=F32)
            for p in range(n_pairs)]
    outs = epilogue(accs, [e[...] for e in extra_refs])
    for o_ref, o in zip(out_refs, outs):
        o_ref[...] = o.astype(o_ref.dtype)


def _fused_matmul(pairs, extras, epilogue, out_dtypes, *, tm, tn, name):
    m = pairs[0][0].shape[0]
    n = pairs[0][1].shape[1]
    assert m % tm == 0 and n % tn == 0, (m, n, tm, tn)
    in_specs, args, block_bytes = [], [], 0
    for x, w in pairs:
        k = x.shape[1]
        assert x.shape == (m, k) and w.shape == (k, n), (x.shape, w.shape)
        in_specs += [pl.BlockSpec((tm, k), lambda i, j: (i, 0)), pl.BlockSpec((k, tn), lambda i, j: (0, j))]
        args += [x, w]
        block_bytes += _nbytes((tm, k), x.dtype) + _nbytes((k, tn), w.dtype)
    for e in extras:
        if e.shape == (m, n):
            in_specs.append(pl.BlockSpec((tm, tn), lambda i, j: (i, j)))
            block_bytes += _nbytes((tm, tn), e.dtype)
        else:
            assert e.shape == (1, n), e.shape
            in_specs.append(pl.BlockSpec((1, tn), lambda i, j: (0, j)))
            block_bytes += _nbytes((8, tn), e.dtype)
        args.append(e)
    out_specs = [pl.BlockSpec((tm, tn), lambda i, j: (i, j)) for _ in out_dtypes]
    out_shape = [jax.ShapeDtypeStruct((m, n), dt) for dt in out_dtypes]
    block_bytes += sum(_nbytes((tm, tn), dt) for dt in out_dtypes)
    block_bytes += len(pairs) * _nbytes((tm, tn), F32)
    kern = functools.partial(_mm_kernel, n_pairs=len(pairs), n_extras=len(extras), epilogue=epilogue)
    return pl.pallas_call(
        kern,
        grid=(m // tm, n // tn),
        in_specs=in_specs,
        out_specs=out_specs,
        out_shape=out_shape,
        compiler_params=pltpu.CompilerParams(
            dimension_semantics=("parallel", "arbitrary"),
            vmem_limit_bytes=_vmem_limit(block_bytes)),
        name=name,
    )(*args)


def _row_tile(m):
    return 1024 if m % 1024 == 0 else 256


def _ln_kernel(x_ref, g_ref, b_ref, o_ref, ob_ref):
    x = x_ref[...]
    mu = jnp.mean(x, axis=-1, keepdims=True)
    xc = x - mu
    var = jnp.mean(xc * xc, axis=-1, keepdims=True)
    y = xc * lax.rsqrt(var + LN_EPS) * g_ref[...] + b_ref[...]
    o_ref[...] = y
    ob_ref[...] = y.astype(BF16)


def _layer_norm(x, g, b, *, name):
    m, d = x.shape
    tr = 256
    row = pl.BlockSpec((tr, d), lambda i: (i, 0))
    vec = pl.BlockSpec((1, d), lambda i: (0, 0))
    return pl.pallas_call(
        _ln_kernel,
        grid=(m // tr,),
        in_specs=[row, vec, vec],
        out_specs=[row, row],
        out_shape=[jax.ShapeDtypeStruct((m, d), F32), jax.ShapeDtypeStruct((m, d), BF16)],
        compiler_params=pltpu.CompilerParams(
            dimension_semantics=("parallel",),
            vmem_limit_bytes=_vmem_limit(_nbytes((tr, d), F32) * 4)),
        name=name,
    )(x, g.reshape(1, d), b.reshape(1, d))


SBP_TQ = 256
SBP_TK = 128


def _sb_prompt_kernel(bias_ref, q_ref, k_ref, vt_ref, tri_ref, o_ref, acc_ref, r_ref, *, tq, tk):
    h = pl.program_id(1)
    i = pl.program_id(2)
    bias = bias_ref[h]
    q = q_ref[...]
    tri = tri_ref[...]
    acc_ref[...] = jnp.zeros_like(acc_ref)
    r_ref[...] = jnp.zeros_like(r_ref)
    q_start = i * tq

    def block(kb, masked):
        k_blk = k_ref[pl.ds(pl.multiple_of(kb * tk, tk), tk), :]
        zt = lax.dot_general(k_blk, q, (((1,), (1,)), ((), ())), preferred_element_type=F32)
        z = zt * SB_SCALE + bias
        log_1m_beta, log_beta = _softplus_parts(z)
        if masked:
            s_idx = kb * tk + lax.broadcasted_iota(jnp.int32, (tk, tq), 0)
            t_idx = q_start + lax.broadcasted_iota(jnp.int32, (tk, tq), 1)
            vis = s_idx < t_idx
            log_1m_beta = jnp.where(vis, log_1m_beta, 0.0)
        hi, lo = _split_bf16(log_1m_beta)
        later_in_block = (jnp.dot(tri, hi, preferred_element_type=F32)
                          + jnp.dot(tri, lo, preferred_element_type=F32))
        r = r_ref[...]
        a = jnp.exp(log_beta + (later_in_block + r))
        if masked:
            a = jnp.where(vis, a, 0.0)
        acc_ref[...] += jnp.dot(vt_ref[kb], a.astype(BF16), preferred_element_type=F32)
        r_ref[...] = r + jnp.sum(log_1m_beta, axis=0, keepdims=True)

    n_diag = tq // tk
    for jj in range(n_diag):
        block(i * n_diag + (n_diag - 1 - jj), True)

    n_off = i * n_diag

    def body(j, carry):
        block(n_off - 1 - j, False)
        return carry

    lax.fori_loop(0, n_off, body, 0)
    o_ref[...] = acc_ref[...].T.astype(o_ref.dtype)


def _sb_prompt(q, k, v, b_sb, batch, seq):
    d, tq, tk = SB_HEAD_DIM, SBP_TQ, SBP_TK
    nq, nkb = seq // tq, seq // tk
    vt = v.reshape(batch, nkb, tk, SB_HEADS, d).transpose(0, 3, 1, 4, 2)
    tri = (jnp.arange(tk)[None, :] > jnp.arange(tk)[:, None]).astype(BF16)
    kern = functools.partial(_sb_prompt_kernel, tq=tq, tk=tk)
    block_bytes = (_nbytes((tq, d), BF16) * 2 + _nbytes((seq, d), BF16) * 2 + _nbytes((tk, tk), BF16)
                   + _nbytes((d, tq), F32) + 6 * _nbytes((tk, tq), F32))
    return pl.pallas_call(
        kern,
        grid_spec=pltpu.PrefetchScalarGridSpec(
            num_scalar_prefetch=1,
            grid=(batch, SB_HEADS, nq),
            in_specs=[
                pl.BlockSpec((tq, d), lambda b, h, i, bias: (b * nq + i, h)),
                pl.BlockSpec((seq, d), lambda b, h, i, bias: (b, h)),
                pl.BlockSpec((None, None, nkb, d, tk), lambda b, h, i, bias: (b, h, 0, 0, 0)),
                pl.BlockSpec((tk, tk), lambda b, h, i, bias: (0, 0)),
            ],
            out_specs=pl.BlockSpec((tq, d), lambda b, h, i, bias: (b * nq + i, h)),
            scratch_shapes=[pltpu.VMEM((d, tq), F32), pltpu.VMEM((1, tq), F32)],
        ),
        out_shape=jax.ShapeDtypeStruct((batch * seq, SB_WIDTH), BF16),
        compiler_params=pltpu.CompilerParams(
            dimension_semantics=("parallel", "parallel", "arbitrary"),
            vmem_limit_bytes=_vmem_limit(block_bytes)),
        name="sb_prompt",
    )(b_sb.astype(F32), q, k, vt, tri)


def _sb_sample_kernel(pt_ref, bias_ref, qbd_ref, kn_ref, vn_ref, kp_ref, vp_ref, tri_ref, o_ref,
                      of_ref, r_ref, *, n_pages, t_new):
    j = pl.program_id(1)
    ps = PAGE_SIZE
    d = SB_HEAD_DIM

    def block(k_blk, v_blk, masked):
        zt = jnp.dot(k_blk, qbd_ref[...], preferred_element_type=F32)
        z = zt * SB_SCALE + bias_ref[...]
        log_1m_beta, log_beta = _softplus_parts(z)
        if masked:
            s_idx = lax.broadcasted_iota(jnp.int32, (ps, LANES), 0)
            t_idx = lax.broadcasted_iota(jnp.int32, (ps, LANES), 1) % t_new
            vis = s_idx < t_idx
            log_1m_beta = jnp.where(vis, log_1m_beta, 0.0)
        hi, lo = _split_bf16(log_1m_beta)
        tri = tri_ref[...]
        later_in_block = jnp.dot(tri, hi, preferred_element_type=F32) + jnp.dot(tri, lo, preferred_element_type=F32)
        r = r_ref[...]
        a = jnp.exp(log_beta + (later_in_block + r))
        if masked:
            a = jnp.where(vis, a, 0.0)
        of_ref[...] += jnp.dot(a.T.astype(BF16), v_blk, preferred_element_type=F32)
        r_ref[...] = r + jnp.sum(log_1m_beta, axis=0, keepdims=True)

    @pl.when(j == 0)
    def _():
        of_ref[...] = jnp.zeros_like(of_ref)
        r_ref[...] = jnp.zeros_like(r_ref)
        block(kn_ref[...], vn_ref[...], True)

    @pl.when(j > 0)
    def _():
        block(kp_ref[...].astype(BF16), vp_ref[...].astype(BF16), False)

    @pl.when(j == n_pages)
    def _():
        for h in range(SB_HEADS):
            o_ref[:, h * d:(h + 1) * d] = of_ref[h * t_new:(h + 1) * t_new, h * d:(h + 1) * d]


def _sb_sample(q, k_new, v_new, cache_k, cache_v, page_table, b_sb, batch, t_new):
    d, ps = SB_HEAD_DIM, PAGE_SIZE
    assert SB_HEADS * t_new == LANES
    n_pages = page_table.shape[1]
    n_phys = cache_k.shape[0]
    q4 = q.reshape(batch, t_new, SB_HEADS, d)
    eye = jnp.eye(SB_HEADS, dtype=BF16)
    qbd = (q4.transpose(0, 2, 3, 1)[:, :, :, None, :] * eye[None, :, None, :, None]).reshape(
        batch, SB_WIDTH, SB_HEADS * t_new)
    pad = ((0, 0), (0, ps - t_new), (0, 0))
    kn = jnp.pad(k_new.reshape(batch, t_new, SB_WIDTH), pad)
    vn = jnp.pad(v_new.reshape(batch, t_new, SB_WIDTH), pad)
    bias_lane = jnp.repeat(b_sb.astype(F32), t_new).reshape(1, LANES)
    tri = (jnp.arange(ps)[None, :] > jnp.arange(ps)[:, None]).astype(BF16)
    ck = cache_k.reshape(n_phys, ps, SB_WIDTH)
    cv = cache_v.reshape(n_phys, ps, SB_WIDTH)

    def page_map(b, j, pt):
        return (pt[b * n_pages + n_pages - jnp.maximum(j, 1)], 0, 0)

    kern = functools.partial(_sb_sample_kernel, n_pages=n_pages, t_new=t_new)
    block_bytes = (_nbytes((ps, SB_WIDTH), F32) * 2 + _nbytes((ps, SB_WIDTH), BF16) * 5
                   + _nbytes((LANES, SB_WIDTH), F32) * 2)
    out = pl.pallas_call(
        kern,
        grid_spec=pltpu.PrefetchScalarGridSpec(
            num_scalar_prefetch=1,
            grid=(batch, n_pages + 1),
            in_specs=[
                pl.BlockSpec((1, LANES), lambda b, j, pt: (0, 0)),
                pl.BlockSpec((None, SB_WIDTH, LANES), lambda b, j, pt: (b, 0, 0)),
                pl.BlockSpec((None, ps, SB_WIDTH), lambda b, j, pt: (b, 0, 0)),
                pl.BlockSpec((None, ps, SB_WIDTH), lambda b, j, pt: (b, 0, 0)),
                pl.BlockSpec((None, ps, SB_WIDTH), page_map),
                pl.BlockSpec((None, ps, SB_WIDTH), page_map),
                pl.BlockSpec((ps, ps), lambda b, j, pt: (0, 0)),
            ],
            out_specs=pl.BlockSpec((None, t_new, SB_WIDTH), lambda b, j, pt: (b, 0, 0)),
            scratch_shapes=[pltpu.VMEM((LANES, SB_WIDTH), F32), pltpu.VMEM((1, LANES), F32)],
        ),
        out_shape=jax.ShapeDtypeStruct((batch, t_new, SB_WIDTH), F32),
        compiler_params=pltpu.CompilerParams(
            dimension_semantics=("parallel", "arbitrary"),
            vmem_limit_bytes=_vmem_limit(block_bytes)),
        name="sb_sample",
    )(page_table.reshape(-1).astype(jnp.int32), bias_lane, qbd, kn, vn, ck, cv, tri)
    return out.reshape(batch * t_new, SB_WIDTH)


def _mlstm_kernel(q_ref, k_ref, v_ref, gi_ref, gf_ref, o_ref, g_ref, tri_ref, c0_ref, n0_ref, m0_ref,
                  h_ref, c_ref, n_ref, m_ref, *, chunk):
    c_idx = pl.program_id(1)
    ln = chunk
    dk, dv = ML_QK_DIM, ML_V_DIM

    @pl.when(c_idx == 0)
    def _():
        c_ref[...] = c0_ref[...]
        n_ref[...] = n0_ref[...]
        m_ref[...] = m0_ref[...]

    i_pre = gi_ref[...]
    log_f = _log_sigmoid(gf_ref[...])
    hi, lo = _split_bf16(log_f)
    tri = tri_ref[...]
    f_cum = jnp.dot(tri, hi, preferred_element_type=F32) + jnp.dot(tri, lo, preferred_element_type=F32)
    a_col = i_pre - f_cum
    rows = lax.broadcasted_iota(jnp.int32, (ln, LANES), 0)
    cmax = a_col
    sh = 1
    while sh < ln:
        cmax = jnp.where(rows >= sh, jnp.maximum(cmax, pltpu.roll(cmax, sh, 0)), cmax)
        sh *= 2
    m_prev = m_ref[...]
    m_all = f_cum + jnp.maximum(m_prev, cmax)
    decay0 = jnp.exp(f_cum + m_prev - m_all)
    inv_floor = jnp.exp(-m_all)
    f_last = f_cum[ln - 1:ln, :]
    m_last = m_all[ln - 1:ln, :]
    w_end = jnp.exp(a_col + (f_last - m_last))
    decay_end = jnp.exp(f_last + m_prev - m_last)
    f_minus_m = f_cum - m_all
    a_row = a_col.T
    causal = (lax.broadcasted_iota(jnp.int32, (ln, ln), 1) <= lax.broadcasted_iota(jnp.int32, (ln, ln), 0))

    for h in range(ML_HEADS):
        qh = q_ref[:, h * dk:(h + 1) * dk]
        kh = k_ref[:, h * dk:(h + 1) * dk]
        vh = v_ref[:, h * dv:(h + 1) * dv]
        log_d = f_minus_m[:, h:h + 1] + a_row[h:h + 1, :]
        dmat = jnp.exp(jnp.where(causal, log_d, -jnp.inf))
        qk = lax.dot_general(qh, kh, (((1,), (1,)), ((), ())), preferred_element_type=F32)
        w = dmat * qk
        c_prev = c_ref[h]
        n_prev = n_ref[h:h + 1, :]
        d0 = decay0[:, h:h + 1]
        num = (jnp.dot(w.astype(BF16), vh, preferred_element_type=F32)
               + jnp.dot(qh, c_prev.astype(BF16), preferred_element_type=F32) * d0)
        den = (jnp.sum(w, axis=-1, keepdims=True)
               + jnp.sum(qh.astype(F32) * n_prev, axis=-1, keepdims=True) * d0)
        hh = num / jnp.maximum(jnp.abs(den), inv_floor[:, h:h + 1])
        mu = jnp.mean(hh, axis=-1, keepdims=True)
        hc = hh - mu
        var = jnp.mean(hc * hc, axis=-1, keepdims=True)
        hn = hc * lax.rsqrt(var + LN_EPS) * g_ref[:, h * dv:(h + 1) * dv]
        h_ref[:, h * dv:(h + 1) * dv] = (hn * jax.nn.sigmoid(o_ref[:, h * dv:(h + 1) * dv])).astype(h_ref.dtype)
        kw = kh.astype(F32) * w_end[:, h:h + 1]
        de = decay_end[:, h:h + 1]
        c_ref[h] = de * c_prev + lax.dot_general(kw.astype(BF16), vh, (((0,), (0,)), ((), ())),
                                                 preferred_element_type=F32)
        n_ref[h:h + 1, :] = de * n_prev + jnp.sum(kw, axis=0, keepdims=True)
    m_ref[...] = m_last


def _mlstm(q, k, v, gi, gf, ml_o, norm_g, c0, n0, m0, n_seq, n_chunks, chunk):
    rows = n_seq * n_chunks * chunk
    tri = (jnp.arange(chunk)[None, :] <= jnp.arange(chunk)[:, None]).astype(BF16)
    row_map = lambda b, c: (b * n_chunks + c, 0)
    kern = functools.partial(_mlstm_kernel, chunk=chunk)
    block_bytes = (_nbytes((chunk, ML_QK_WIDTH), BF16) * 2 + _nbytes((chunk, ML_V_WIDTH), BF16) * 2
                   + _nbytes((chunk, LANES), F32) * 2 + _nbytes((chunk, ML_V_WIDTH), F32)
                   + 2 * _nbytes((ML_HEADS, ML_QK_DIM, ML_V_DIM), F32))
    return pl.pallas_call(
        kern,
        grid=(n_seq, n_chunks),
        in_specs=[
            pl.BlockSpec((chunk, ML_QK_WIDTH), row_map),
            pl.BlockSpec((chunk, ML_QK_WIDTH), row_map),
            pl.BlockSpec((chunk, ML_V_WIDTH), row_map),
            pl.BlockSpec((chunk, LANES), row_map),
            pl.BlockSpec((chunk, LANES), row_map),
            pl.BlockSpec((chunk, ML_V_WIDTH), row_map),
            pl.BlockSpec((1, ML_V_WIDTH), lambda b, c: (0, 0)),
            pl.BlockSpec((chunk, chunk), lambda b, c: (0, 0)),
            pl.BlockSpec((None, ML_HEADS, ML_QK_DIM, ML_V_DIM), lambda b, c: (b, 0, 0, 0)),
            pl.BlockSpec((None, ML_HEADS, ML_QK_DIM), lambda b, c: (b, 0, 0)),
            pl.BlockSpec((None, 1, LANES), lambda b, c: (b, 0, 0)),
        ],
        out_specs=[
            pl.BlockSpec((chunk, ML_V_WIDTH), row_map),
            pl.BlockSpec((None, ML_HEADS, ML_QK_DIM, ML_V_DIM), lambda b, c: (b, 0, 0, 0)),
            pl.BlockSpec((None, ML_HEADS, ML_QK_DIM), lambda b, c: (b, 0, 0)),
            pl.BlockSpec((None, 1, LANES), lambda b, c: (b, 0, 0)),
        ],
        out_shape=[
            jax.ShapeDtypeStruct((rows, ML_V_WIDTH), BF16),
            jax.ShapeDtypeStruct((n_seq, ML_HEADS, ML_QK_DIM, ML_V_DIM), F32),
            jax.ShapeDtypeStruct((n_seq, ML_HEADS, ML_QK_DIM), F32),
            jax.ShapeDtypeStruct((n_seq, 1, LANES), F32),
        ],
        compiler_params=pltpu.CompilerParams(
            dimension_semantics=("parallel", "arbitrary"),
            vmem_limit_bytes=_vmem_limit(block_bytes)),
        name="mlstm",
    )(q, k, v, gi, gf, ml_o, norm_g.reshape(1, ML_V_WIDTH), tri, c0, n0, m0)


def _split_w_in(w_in, b_igate, b_fgate):
    o = 0
    cols = {}
    for name, width in (("sb_q", SB_WIDTH), ("sb_k", SB_WIDTH), ("sb_v", SB_WIDTH),
                        ("ml_q", ML_QK_WIDTH), ("ml_k", ML_QK_WIDTH), ("ml_v", ML_V_WIDTH),
                        ("ml_o", ML_V_WIDTH), ("ml_i", ML_HEADS), ("ml_f", ML_HEADS),
                        ("g_sb", D_MODEL), ("g_ml", D_MODEL)):
        cols[name] = w_in[:, o:o + width].astype(BF16)
        o += width
    lane_pad = ((0, 0), (0, LANES - ML_HEADS))
    cols["ml_i"] = jnp.pad(cols["ml_i"], lane_pad)
    cols["ml_f"] = jnp.pad(cols["ml_f"], lane_pad)
    bias_i = jnp.pad(b_igate.astype(F32).reshape(1, ML_HEADS), lane_pad)
    bias_f = jnp.pad(b_fgate.astype(F32).reshape(1, ML_HEADS), lane_pad)
    return cols, bias_i, bias_f


def _project(xb, w, out_dtypes, name, *, scale=None, bias=None):
    tm = _row_tile(xb.shape[0])
    tn = min(512, w.shape[1])

    def epilogue(accs, extras):
        y = accs[0]
        if scale is not None:
            y = y * scale
        if bias is not None:
            y = y + extras[0]
        return [y] * len(out_dtypes)

    extras = [] if bias is None else [bias]
    return _fused_matmul([(xb, w)], extras, epilogue, out_dtypes, tm=tm, tn=tn, name=name)


def _branch_inputs(xb, wc, bias_i, bias_f):
    (q_sb,) = _project(xb, wc["sb_q"], [BF16], "proj_sb_q")
    k_f, k_b = _project(xb, wc["sb_k"], [F32, BF16], "proj_sb_k")
    v_f, v_b = _project(xb, wc["sb_v"], [F32, BF16], "proj_sb_v")
    (q_ml,) = _project(xb, wc["ml_q"], [BF16], "proj_ml_q")
    (k_ml,) = _project(xb, wc["ml_k"], [BF16], "proj_ml_k", scale=ML_QK_DIM ** -0.5)
    (v_ml,) = _project(xb, wc["ml_v"], [BF16], "proj_ml_v")
    (o_ml,) = _project(xb, wc["ml_o"], [F32], "proj_ml_o")
    (gi,) = _project(xb, wc["ml_i"], [F32], "proj_ml_i", bias=bias_i)
    (gf,) = _project(xb, wc["ml_f"], [F32], "proj_ml_f", bias=bias_f)
    (g_sb,) = _project(xb, wc["g_sb"], [F32], "proj_g_sb")
    (g_ml,) = _project(xb, wc["g_ml"], [F32], "proj_g_ml")
    return q_sb, (k_f, k_b), (v_f, v_b), (q_ml, k_ml, v_ml), o_ml, (gi, gf), (g_sb, g_ml)


def _merge_and_ffn(x, o_sb, h_ml, g_sb, g_ml, p, w):
    m = x.shape[0]
    tm = _row_tile(m)

    def merge_epi(accs, extras):
        return [jax.nn.sigmoid(extras[0]) * accs[0] + jax.nn.sigmoid(extras[1]) * accs[1]]

    (mix,) = _fused_matmul([(o_sb, w["sb_proj"]), (h_ml, w["ml_proj"])], [g_sb, g_ml], merge_epi, [BF16],
                           tm=tm, tn=512, name="merge")

    def resid_epi(accs, extras):
        return [DEEPNORM_ALPHA * extras[0] + accs[0]]

    (r1,) = _fused_matmul([(mix, w["out"])], [x], resid_epi, [F32], tm=tm, tn=512, name="out_proj")
    x1, x1b = _layer_norm(r1, w["ln1_g"], w["ln1_b"], name="ln1")

    def swiglu_epi(accs, extras):
        g = accs[0]
        return [g * jax.nn.sigmoid(g) * accs[1]]

    (act,) = _fused_matmul([(x1b, w["ffn_gate"]), (x1b, w["ffn_up"])], [], swiglu_epi, [BF16],
                           tm=tm, tn=512, name="ffn_up")
    (r2,) = _fused_matmul([(act, w["ffn_down"])], [x1], resid_epi, [F32],
                          tm=min(tm, 512), tn=256, name="ffn_down")
    x2, x2b = _layer_norm(r2, w["ln2_g"], w["ln2_b"], name="ln2")

    def ple_epi(accs, extras):
        return [extras[0] + accs[1] * jax.nn.sigmoid(accs[0])]

    (y,) = _fused_matmul([(x2b, w["ple_gate"]), (p.astype(BF16), w["ple"])], [x2], ple_epi, [F32],
                         tm=tm, tn=512, name="ple")
    return y


def kernel(x_prompt, x_sample, cache_k, cache_v, state_C, state_n, state_m, page_table, p_prompt, p_sample,
           w_in, b_igate, b_fgate, b_sb, ml_norm_g, w_sb_proj, w_ml_proj, w_out, ln1_g, ln1_b,
           w_ffn_gate, w_ffn_up, w_ffn_down, ln2_g, ln2_b, w_ple, w_ple_gate):
    assert w_in.shape[0] == DEPTH == 1
    batch, seq, _ = x_prompt.shape
    dec_batch, dec_seq, _ = x_sample.shape
    sdt = state_C.dtype
    l = 0

    wc, bias_i, bias_f = _split_w_in(w_in[l], b_igate[l], b_fgate[l])
    hid_pad = FFN_HIDDEN_PAD - FFN_HIDDEN
    w = {
        "sb_proj": w_sb_proj[l].astype(BF16),
        "ml_proj": w_ml_proj[l].astype(BF16),
        "out": w_out[l].astype(BF16),
        "ffn_gate": jnp.pad(w_ffn_gate[l].astype(BF16), ((0, 0), (0, hid_pad))),
        "ffn_up": jnp.pad(w_ffn_up[l].astype(BF16), ((0, 0), (0, hid_pad))),
        "ffn_down": jnp.pad(w_ffn_down[l].astype(BF16), ((0, hid_pad), (0, 0))),
        "ple": w_ple[l].astype(BF16),
        "ple_gate": w_ple_gate[l].astype(BF16),
        "ln1_g": ln1_g[l], "ln1_b": ln1_b[l], "ln2_g": ln2_g[l], "ln2_b": ln2_b[l],
    }

    mp = batch * seq
    xp = x_prompt.reshape(mp, D_MODEL)
    q_sb, (k_f, k_b), (v_f, v_b), (q_ml, k_ml, v_ml), o_ml, (gi, gf), (g_sb, g_ml) = _branch_inputs(
        xp.astype(BF16), wc, bias_i, bias_f)
    o_sb = _sb_prompt(q_sb, k_b, v_b, b_sb[l], batch, seq)
    zeros_c = jnp.zeros((batch, ML_HEADS, ML_QK_DIM, ML_V_DIM), F32)
    zeros_n = jnp.zeros((batch, ML_HEADS, ML_QK_DIM), F32)
    zeros_m = jnp.zeros((batch, 1, LANES), F32)
    h_ml, c_p, n_p, m_p = _mlstm(q_ml, k_ml, v_ml, gi, gf, o_ml, ml_norm_g[l], zeros_c, zeros_n, zeros_m,
                                 batch, seq // ML_CHUNK, ML_CHUNK)
    y_prompt = _merge_and_ffn(xp, o_sb, h_ml, g_sb, g_ml, p_prompt[l].reshape(mp, -1), w)
    k_prompt = k_f.reshape(1, batch, seq, SB_HEADS, SB_HEAD_DIM)
    v_prompt = v_f.reshape(1, batch, seq, SB_HEADS, SB_HEAD_DIM)
    c_prompt = c_p.astype(sdt)[None]
    n_prompt = n_p.astype(sdt)[None]
    m_prompt = m_p[:, 0, :ML_HEADS].astype(sdt)[None]

    ms = dec_batch * dec_seq
    xs = x_sample.reshape(ms, D_MODEL)
    q_sb, (k_f, k_b), (v_f, v_b), (q_ml, k_ml, v_ml), o_ml, (gi, gf), (g_sb, g_ml) = _branch_inputs(
        xs.astype(BF16), wc, bias_i, bias_f)
    o_sb = _sb_sample(q_sb, k_b, v_b, cache_k[l], cache_v[l], page_table, b_sb[l], dec_batch, dec_seq)

    def pad_chunk(a, value=0.0):
        a3 = a.reshape(dec_batch, dec_seq, a.shape[-1])
        a3 = jnp.pad(a3, ((0, 0), (0, ML_CHUNK - dec_seq), (0, 0)), constant_values=value)
        return a3.reshape(dec_batch * ML_CHUNK, a.shape[-1])

    m0 = jnp.pad(state_m[l].astype(F32), ((0, 0), (0, LANES - ML_HEADS))).reshape(dec_batch, 1, LANES)
    h_pad, c_s, n_s, m_s = _mlstm(
        pad_chunk(q_ml), pad_chunk(k_ml), pad_chunk(v_ml), pad_chunk(gi, PAD_GATE_NEG), pad_chunk(gf, PAD_GATE_POS),
        pad_chunk(o_ml), ml_norm_g[l], state_C[l].astype(F32), state_n[l].astype(F32), m0,
        dec_batch, 1, ML_CHUNK)
    h_ml = h_pad.reshape(dec_batch, ML_CHUNK, ML_V_WIDTH)[:, :dec_seq].reshape(ms, ML_V_WIDTH)
    y_sample = _merge_and_ffn(xs, o_sb.astype(BF16), h_ml, g_sb, g_ml, p_sample[l].reshape(ms, -1), w)
    k_sample = k_f.reshape(1, dec_batch, dec_seq, SB_HEADS, SB_HEAD_DIM)
    v_sample = v_f.reshape(1, dec_batch, dec_seq, SB_HEADS, SB_HEAD_DIM)
    c_sample = c_s.astype(sdt)[None]
    n_sample = n_s.astype(sdt)[None]
    m_sample = m_s[:, 0, :ML_HEADS].astype(sdt)[None]

    return (y_prompt.reshape(batch, seq, D_MODEL), y_sample.reshape(dec_batch, dec_seq, D_MODEL),
            k_prompt, v_prompt, c_prompt, n_prompt, m_prompt,
            k_sample, v_sample, c_sample, n_sample, m_sample)
```

```python
import functools
import math

import jax
import jax.numpy as jnp
from jax import lax
from jax.experimental import pallas as pl
from jax.experimental.pallas import tpu as pltpu

F32 = jnp.float32
BF16 = jnp.bfloat16

D_MODEL = 4096
SB_HEADS = 16
SB_HEAD_DIM = 128
SB_WIDTH = SB_HEADS * SB_HEAD_DIM
SB_SCALE = SB_HEAD_DIM ** -0.5
PAGE_SIZE = 128
ML_HEADS = 8
ML_QK_DIM = 128
ML_V_DIM = 256
ML_QK_WIDTH = ML_HEADS * ML_QK_DIM
ML_V_WIDTH = ML_HEADS * ML_V_DIM
ML_CHUNK = 128
LN_EPS = 1e-5
DEPTH = 1
DEEPNORM_ALPHA = (2 * DEPTH) ** 0.25

LANES = 128
V7X_VMEM_BYTES = 64 * 1024 * 1024
VMEM_LIMIT_CAP = V7X_VMEM_BYTES - 8 * 1024 * 1024
PAD_GATE_NEG = -1e30
PAD_GATE_POS = 1e30


def _vmem_limit(block_bytes):
    need = 2 * block_bytes + 8 * 1024 * 1024
    return int(min(max(need, 16 * 1024 * 1024), VMEM_LIMIT_CAP))


def _nbytes(shape, dtype):
    return math.prod(shape) * jnp.dtype(dtype).itemsize


def _softplus_parts(z):
    sp = jnp.maximum(z, 0.0) + jnp.log(1.0 + jnp.exp(-jnp.abs(z)))
    return sp, z - sp


def _log_sigmoid(z):
    return jnp.minimum(z, 0.0) - jnp.log1p(jnp.exp(-jnp.abs(z)))


def _split_bf16(x):
    hi = x.astype(BF16)
    lo = (x - hi.astype(F32)).astype(BF16)
    return hi, lo


def _mm_kernel(*refs, n_pairs, n_extras, epilogue):
    pair_refs = refs[:2 * n_pairs]
    extra_refs = refs[2 * n_pairs:2 * n_pairs + n_extras]
    out_refs = refs[2 * n_pairs + n_extras:]
    accs = [jnp.dot(pair_refs[2 * p][...], pair_refs[2 * p + 1][...], preferred_element_type=F32)
            for p in range(n_pairs)]
    outs = epilogue(accs, [e[...] for e in extra_refs])
    for o_ref, o in zip(out_refs, outs):
        o_ref[...] = o.astype(o_ref.dtype)


def _fused_matmul(pairs, extras, epilogue, out_dtypes, *, tm, tn, name):
    m = pairs[0][0].shape[0]
    n = pairs[0][1].shape[1]
    assert m % tm == 0 and n % tn == 0, (m, n, tm, tn)
    in_specs, args, block_bytes = [], [], 0
    for x, w in pairs:
        k = x.shape[1]
        assert x.shape == (m, k) and w.shape == (k, n), (x.shape, w.shape)
        in_specs += [pl.BlockSpec((tm, k), lambda i, j: (i, 0)), pl.BlockSpec((k, tn), lambda i, j: (0, j))]
        args += [x, w]
        block_bytes += _nbytes((tm, k), x.dtype) + _nbytes((k, tn), w.dtype)
    for e in extras:
        if e.shape == (m, n):
            in_specs.append(pl.BlockSpec((tm, tn), lambda i, j: (i, j)))
            block_bytes += _nbytes((tm, tn), e.dtype)
        else:
            assert e.shape == (1, n), e.shape
            in_specs.append(pl.BlockSpec((1, tn), lambda i, j: (0, j)))
            block_bytes += _nbytes((8, tn), e.dtype)
        args.append(e)
    out_specs = [pl.BlockSpec((tm, tn), lambda i, j: (i, j)) for _ in out_dtypes]
    out_shape = [jax.ShapeDtypeStruct((m, n), dt) for dt in out_dtypes]
    block_bytes += sum(_nbytes((tm, tn), dt) for dt in out_dtypes)
    block_bytes += len(pairs) * _nbytes((tm, tn), F32)
    kern = functools.partial(_mm_kernel, n_pairs=len(pairs), n_extras=len(extras), epilogue=epilogue)
    return pl.pallas_call(
        kern,
        grid=(m // tm, n // tn),
        in_specs=in_specs,
        out_specs=out_specs,
        out_shape=out_shape,
        compiler_params=pltpu.CompilerParams(
            dimension_semantics=("parallel", "arbitrary"),
            vmem_limit_bytes=_vmem_limit(block_bytes)),
        name=name,
    )(*args)


def _row_tile(m):
    return 1024 if m % 1024 == 0 else 256


def _ln_kernel(x_ref, g_ref, b_ref, o_ref, ob_ref):
    x = x_ref[...]
    mu = jnp.mean(x, axis=-1, keepdims=True)
    xc = x - mu
    var = jnp.mean(xc * xc, axis=-1, keepdims=True)
    y = xc * lax.rsqrt(var + LN_EPS) * g_ref[...] + b_ref[...]
    o_ref[...] = y
    ob_ref[...] = y.astype(BF16)


def _layer_norm(x, g, b, *, name):
    m, d = x.shape
    tr = 256
    row = pl.BlockSpec((tr, d), lambda i: (i, 0))
    vec = pl.BlockSpec((1, d), lambda i: (0, 0))
    return pl.pallas_call(
        _ln_kernel,
        grid=(m // tr,),
        in_specs=[row, vec, vec],
        out_specs=[row, row],
        out_shape=[jax.ShapeDtypeStruct((m, d), F32), jax.ShapeDtypeStruct((m, d), BF16)],
        compiler_params=pltpu.CompilerParams(
            dimension_semantics=("parallel",),
            vmem_limit_bytes=_vmem_limit(_nbytes((tr, d), F32) * 4)),
        name=name,
    )(x, g.reshape(1, d), b.reshape(1, d))


SBP_TILE = 512
SB_TK = 128


def _later_matrix():
    s = jnp.arange(SB_TK + 8)[:, None]
    s2 = jnp.arange(2 * SB_TK)[None, :] % SB_TK
    return ((s2 > s) | (s >= SB_TK)).astype(BF16)


def _later_sums(w, later_mat, r):
    hi, lo = _split_bf16(w)
    n_sub = w.shape[0] // SB_TK
    later = [None] * n_sub
    for m in reversed(range(n_sub)):
        sl = slice(m * SB_TK, (m + 1) * SB_TK)
        sums = jnp.dot(later_mat, jnp.concatenate([hi[sl], lo[sl]], axis=0), preferred_element_type=F32)
        later[m] = sums[:SB_TK] + r
        r = r + sums[SB_TK:SB_TK + 1]
    return (later[0] if n_sub == 1 else jnp.concatenate(later, axis=0)), r


def _sb_prompt_kernel(bias_ref, q_ref, k_ref, v_ref, later_ref, o_ref, vt_ref, acc_ref, r_ref, *, tile, seq):
    h = pl.program_id(1)
    i = pl.program_id(2)

    @pl.when(i == 0)
    def _():
        for c in range(seq // tile):
            vt_ref[c] = v_ref[c * tile:(c + 1) * tile, :].astype(F32).T.astype(BF16)

    bias = bias_ref[h]
    q = q_ref[...]
    later_mat = later_ref[...]
    acc_ref[...] = jnp.zeros_like(acc_ref)
    r_ref[...] = jnp.zeros_like(r_ref)

    def key_tile(c, masked):
        k_c = k_ref[pl.ds(pl.multiple_of(c * tile, tile), tile), :]
        zt = lax.dot_general(k_c, q, (((1,), (1,)), ((), ())), preferred_element_type=F32)
        z = zt * SB_SCALE + bias
        neg_log_1m_beta, log_beta = _softplus_parts(z)
        if masked:
            vis = (lax.broadcasted_iota(jnp.int32, (tile, tile), 0)
                   < lax.broadcasted_iota(jnp.int32, (tile, tile), 1))
            neg_log_1m_beta = jnp.where(vis, neg_log_1m_beta, 0.0)
        later, r = _later_sums(neg_log_1m_beta, later_mat, r_ref[...])
        r_ref[...] = r
        a = jnp.exp(log_beta - later)
        if masked:
            a = jnp.where(vis, a, 0.0)
        acc_ref[...] += jnp.dot(vt_ref[c], a.astype(BF16), preferred_element_type=F32)

    key_tile(i, True)

    def body(j, carry):
        key_tile(i - 1 - j, False)
        return carry

    lax.fori_loop(0, i, body, 0)
    o_ref[...] = acc_ref[...].T.astype(o_ref.dtype)


def _sb_prompt(q, k, v, b_sb, batch, seq):
    d, tile = SB_HEAD_DIM, SBP_TILE
    nq = seq // tile
    later_mat = _later_matrix()
    kern = functools.partial(_sb_prompt_kernel, tile=tile, seq=seq)
    block_bytes = (_nbytes((tile, d), BF16) * 2 + _nbytes((seq, d), BF16) * 3 + _nbytes((d, tile), F32)
                   + 10 * _nbytes((tile, tile), F32))
    return pl.pallas_call(
        kern,
        grid_spec=pltpu.PrefetchScalarGridSpec(
            num_scalar_prefetch=1,
            grid=(batch, SB_HEADS, nq),
            in_specs=[
                pl.BlockSpec((tile, d), lambda b, h, i, bias: (b * nq + i, h)),
                pl.BlockSpec((seq, d), lambda b, h, i, bias: (b, h)),
                pl.BlockSpec((seq, d), lambda b, h, i, bias: (b, h)),
                pl.BlockSpec(later_mat.shape, lambda b, h, i, bias: (0, 0)),
            ],
            out_specs=pl.BlockSpec((tile, d), lambda b, h, i, bias: (b * nq + i, h)),
            scratch_shapes=[pltpu.VMEM((nq, d, tile), BF16), pltpu.VMEM((d, tile), F32),
                            pltpu.VMEM((1, tile), F32)],
        ),
        out_shape=jax.ShapeDtypeStruct((batch * seq, SB_WIDTH), BF16),
        compiler_params=pltpu.CompilerParams(
            dimension_semantics=("arbitrary", "arbitrary", "arbitrary"),
            vmem_limit_bytes=_vmem_limit(block_bytes)),
        name="sb_prompt",
    )(b_sb.astype(F32), q, k, v, later_mat)


SBS_PAGES = 4


def _sb_sample_kernel(pt_ref, bias_ref, qbd_ref, kn_ref, vn_ref, *rest, n_steps, t_new):
    kp_refs = rest[:SBS_PAGES]
    vp_refs = rest[SBS_PAGES:2 * SBS_PAGES]
    later_ref, o_ref, k2_ref, v2_ref, of_ref, r_ref = rest[2 * SBS_PAGES:]
    j = pl.program_id(1)
    ps = PAGE_SIZE
    d = SB_HEAD_DIM

    def attend(k_blk, v_blk, masked):
        zt = jnp.dot(k_blk, qbd_ref[...], preferred_element_type=F32)
        z = zt * SB_SCALE + bias_ref[...]
        neg_log_1m_beta, log_beta = _softplus_parts(z)
        if masked:
            s_idx = lax.broadcasted_iota(jnp.int32, z.shape, 0)
            t_idx = lax.broadcasted_iota(jnp.int32, z.shape, 1) % t_new
            vis = s_idx < t_idx
            neg_log_1m_beta = jnp.where(vis, neg_log_1m_beta, 0.0)
        later, r = _later_sums(neg_log_1m_beta, later_ref[...], r_ref[...])
        r_ref[...] = r
        a = jnp.exp(log_beta - later)
        if masked:
            a = jnp.where(vis, a, 0.0)
        of_ref[...] += jnp.dot(a.T.astype(BF16), v_blk, preferred_element_type=F32)

    @pl.when(j == 0)
    def _():
        of_ref[...] = jnp.zeros_like(of_ref)
        r_ref[...] = jnp.zeros_like(r_ref)
        attend(kn_ref[...], vn_ref[...], True)

    @pl.when(j > 0)
    def _():
        for m in range(SBS_PAGES):
            for h in range(SB_HEADS):
                head_rows = pl.ds(h, ps, stride=SB_HEADS)
                k2_ref[m * ps:(m + 1) * ps, h * d:(h + 1) * d] = kp_refs[m][head_rows, :].astype(BF16)
                v2_ref[m * ps:(m + 1) * ps, h * d:(h + 1) * d] = vp_refs[m][head_rows, :].astype(BF16)
        attend(k2_ref[...], v2_ref[...], False)

    @pl.when(j == n_steps)
    def _():
        for h in range(SB_HEADS):
            o_ref[:, h * d:(h + 1) * d] = of_ref[h * t_new:(h + 1) * t_new, h * d:(h + 1) * d]


def _sb_sample(q, k_new, v_new, cache_k, cache_v, page_table, b_sb, batch, t_new):
    d, ps = SB_HEAD_DIM, PAGE_SIZE
    assert SB_HEADS * t_new == LANES
    n_pages = page_table.shape[1]
    n_phys = cache_k.shape[0]
    q4 = q.reshape(batch, t_new, SB_HEADS, d)
    eye = jnp.eye(SB_HEADS, dtype=BF16)
    qbd = (q4.transpose(0, 2, 3, 1)[:, :, :, None, :] * eye[None, :, None, :, None]).reshape(
        batch, SB_WIDTH, SB_HEADS * t_new)
    pad = ((0, 0), (0, ps - t_new), (0, 0))
    kn = jnp.pad(k_new.reshape(batch, t_new, SB_WIDTH), pad)
    vn = jnp.pad(v_new.reshape(batch, t_new, SB_WIDTH), pad)
    bias_lane = jnp.repeat(b_sb.astype(F32), t_new).reshape(1, LANES)
    later_mat = _later_matrix()
    assert ps == SB_TK and n_pages % SBS_PAGES == 0
    n_steps = n_pages // SBS_PAGES
    rows = SBS_PAGES * ps

    def page_map(m):
        def index_map(b, j, pt):
            return (pt[b * n_pages + n_pages - jnp.maximum(j, 1) * SBS_PAGES + m], 0, 0)
        return index_map

    ck = cache_k.reshape(n_phys, ps * SB_HEADS, d)
    cv = cache_v.reshape(n_phys, ps * SB_HEADS, d)
    page_specs = [pl.BlockSpec((None, ps * SB_HEADS, d), page_map(m)) for m in range(SBS_PAGES)]
    kern = functools.partial(_sb_sample_kernel, n_steps=n_steps, t_new=t_new)
    block_bytes = (2 * SBS_PAGES * _nbytes((ps, SB_WIDTH), F32) + 4 * _nbytes((rows, SB_WIDTH), BF16)
                   + _nbytes((LANES, SB_WIDTH), F32) * 2)
    out = pl.pallas_call(
        kern,
        grid_spec=pltpu.PrefetchScalarGridSpec(
            num_scalar_prefetch=1,
            grid=(batch, n_steps + 1),
            in_specs=[
                pl.BlockSpec((1, LANES), lambda b, j, pt: (0, 0)),
                pl.BlockSpec((None, SB_WIDTH, LANES), lambda b, j, pt: (b, 0, 0)),
                pl.BlockSpec((None, ps, SB_WIDTH), lambda b, j, pt: (b, 0, 0)),
                pl.BlockSpec((None, ps, SB_WIDTH), lambda b, j, pt: (b, 0, 0)),
                *page_specs,
                *page_specs,
                pl.BlockSpec(later_mat.shape, lambda b, j, pt: (0, 0)),
            ],
            out_specs=pl.BlockSpec((None, t_new, SB_WIDTH), lambda b, j, pt: (b, 0, 0)),
            scratch_shapes=[pltpu.VMEM((rows, SB_WIDTH), BF16), pltpu.VMEM((rows, SB_WIDTH), BF16),
                            pltpu.VMEM((LANES, SB_WIDTH), F32), pltpu.VMEM((1, LANES), F32)],
        ),
        out_shape=jax.ShapeDtypeStruct((batch, t_new, SB_WIDTH), F32),
        compiler_params=pltpu.CompilerParams(
            dimension_semantics=("arbitrary", "arbitrary"),
            vmem_limit_bytes=_vmem_limit(block_bytes)),
        name="sb_sample",
    )(page_table.reshape(-1).astype(jnp.int32), bias_lane, qbd, kn, vn,
      *([ck] * SBS_PAGES), *([cv] * SBS_PAGES), later_mat)
    return out.reshape(batch * t_new, SB_WIDTH)


def _mlstm_kernel(q_ref, k_ref, v_ref, gi_ref, gf_ref, o_ref, g_ref, tri_ref, c0_ref, n0_ref, m0_ref,
                  h_ref, c_ref, n_ref, m_ref, *, chunk):
    c_idx = pl.program_id(1)
    ln = chunk
    dk, dv = ML_QK_DIM, ML_V_DIM

    @pl.when(c_idx == 0)
    def _():
        c_ref[...] = c0_ref[...]
        n_ref[...] = n0_ref[...]
        m_ref[...] = m0_ref[...]

    i_pre = gi_ref[...]
    log_f = _log_sigmoid(gf_ref[...])
    hi, lo = _split_bf16(log_f)
    tri = tri_ref[...]
    f_cum = jnp.dot(tri, hi, preferred_element_type=F32) + jnp.dot(tri, lo, preferred_element_type=F32)
    a_col = i_pre - f_cum
    rows = lax.broadcasted_iota(jnp.int32, (ln, LANES), 0)
    cmax = a_col
    sh = 1
    while sh < ln:
        cmax = jnp.where(rows >= sh, jnp.maximum(cmax, pltpu.roll(cmax, sh, 0)), cmax)
        sh *= 2
    m_prev = m_ref[...]
    m_all = f_cum + jnp.maximum(m_prev, cmax)
    decay0 = jnp.exp(f_cum + m_prev - m_all)
    inv_floor = jnp.exp(-m_all)
    f_last = f_cum[ln - 1:ln, :]
    m_last = m_all[ln - 1:ln, :]
    w_end = jnp.exp(a_col + (f_last - m_last))
    decay_end = jnp.exp(f_last + m_prev - m_last)
    f_minus_m = f_cum - m_all
    a_row = a_col.T
    causal = (lax.broadcasted_iota(jnp.int32, (ln, ln), 1) <= lax.broadcasted_iota(jnp.int32, (ln, ln), 0))

    for h in range(ML_HEADS):
        qh = q_ref[:, h * dk:(h + 1) * dk]
        kh = k_ref[:, h * dk:(h + 1) * dk]
        vh = v_ref[:, h * dv:(h + 1) * dv]
        log_d = f_minus_m[:, h:h + 1] + a_row[h:h + 1, :]
        dmat = jnp.exp(jnp.where(causal, log_d, -jnp.inf))
        qk = lax.dot_general(qh, kh, (((1,), (1,)), ((), ())), preferred_element_type=F32)
        w = dmat * qk
        c_prev = c_ref[h]
        n_prev = n_ref[h:h + 1, :]
        d0 = decay0[:, h:h + 1]
        num = (jnp.dot(w.astype(BF16), vh, preferred_element_type=F32)
               + jnp.dot(qh, c_prev.astype(BF16), preferred_element_type=F32) * d0)
        den = (jnp.sum(w, axis=-1, keepdims=True)
               + jnp.sum(qh.astype(F32) * n_prev, axis=-1, keepdims=True) * d0)
        hh = num / jnp.maximum(jnp.abs(den), inv_floor[:, h:h + 1])
        mu = jnp.mean(hh, axis=-1, keepdims=True)
        hc = hh - mu
        var = jnp.mean(hc * hc, axis=-1, keepdims=True)
        hn = hc * lax.rsqrt(var + LN_EPS) * g_ref[:, h * dv:(h + 1) * dv]
        h_ref[:, h * dv:(h + 1) * dv] = (hn * jax.nn.sigmoid(o_ref[:, h * dv:(h + 1) * dv])).astype(h_ref.dtype)
        kw = kh.astype(F32) * w_end[:, h:h + 1]
        de = decay_end[:, h:h + 1]
        c_ref[h] = de * c_prev + lax.dot_general(kw.astype(BF16), vh, (((0,), (0,)), ((), ())),
                                                 preferred_element_type=F32)
        n_ref[h:h + 1, :] = de * n_prev + jnp.sum(kw, axis=0, keepdims=True)
    m_ref[...] = m_last


def _mlstm(q, k, v, gi, gf, ml_o, norm_g, c0, n0, m0, n_seq, n_chunks, chunk):
    rows = n_seq * n_chunks * chunk
    tri = (jnp.arange(chunk)[None, :] <= jnp.arange(chunk)[:, None]).astype(BF16)
    row_map = lambda b, c: (b * n_chunks + c, 0)
    kern = functools.partial(_mlstm_kernel, chunk=chunk)
    block_bytes = (_nbytes((chunk, ML_QK_WIDTH), BF16) * 2 + _nbytes((chunk, ML_V_WIDTH), BF16) * 2
                   + _nbytes((chunk, LANES), F32) * 2 + _nbytes((chunk, ML_V_WIDTH), F32)
                   + 2 * _nbytes((ML_HEADS, ML_QK_DIM, ML_V_DIM), F32))
    return pl.pallas_call(
        kern,
        grid=(n_seq, n_chunks),
        in_specs=[
            pl.BlockSpec((chunk, ML_QK_WIDTH), row_map),
            pl.BlockSpec((chunk, ML_QK_WIDTH), row_map),
            pl.BlockSpec((chunk, ML_V_WIDTH), row_map),
            pl.BlockSpec((chunk, LANES), row_map),
            pl.BlockSpec((chunk, LANES), row_map),
            pl.BlockSpec((chunk, ML_V_WIDTH), row_map),
            pl.BlockSpec((1, ML_V_WIDTH), lambda b, c: (0, 0)),
            pl.BlockSpec((chunk, chunk), lambda b, c: (0, 0)),
            pl.BlockSpec((None, ML_HEADS, ML_QK_DIM, ML_V_DIM), lambda b, c: (b, 0, 0, 0)),
            pl.BlockSpec((None, ML_HEADS, ML_QK_DIM), lambda b, c: (b, 0, 0)),
            pl.BlockSpec((None, 1, LANES), lambda b, c: (b, 0, 0)),
        ],
        out_specs=[
            pl.BlockSpec((chunk, ML_V_WIDTH), row_map),
            pl.BlockSpec((None, ML_HEADS, ML_QK_DIM, ML_V_DIM), lambda b, c: (b, 0, 0, 0)),
            pl.BlockSpec((None, ML_HEADS, ML_QK_DIM), lambda b, c: (b, 0, 0)),
            pl.BlockSpec((None, 1, LANES), lambda b, c: (b, 0, 0)),
        ],
        out_shape=[
            jax.ShapeDtypeStruct((rows, ML_V_WIDTH), BF16),
            jax.ShapeDtypeStruct((n_seq, ML_HEADS, ML_QK_DIM, ML_V_DIM), F32),
            jax.ShapeDtypeStruct((n_seq, ML_HEADS, ML_QK_DIM), F32),
            jax.ShapeDtypeStruct((n_seq, 1, LANES), F32),
        ],
        compiler_params=pltpu.CompilerParams(
            dimension_semantics=("parallel", "arbitrary"),
            vmem_limit_bytes=_vmem_limit(block_bytes)),
        name="mlstm",
    )(q, k, v, gi, gf, ml_o, norm_g.reshape(1, ML_V_WIDTH), tri, c0, n0, m0)


def _split_w_in(w_in, b_igate, b_fgate):
    o = 0
    cols = {}
    for name, width in (("sb_q", SB_WIDTH), ("sb_k", SB_WIDTH), ("sb_v", SB_WIDTH),
                        ("ml_q", ML_QK_WIDTH), ("ml_k", ML_QK_WIDTH), ("ml_v", ML_V_WIDTH),
                        ("ml_o", ML_V_WIDTH), ("ml_i", ML_HEADS), ("ml_f", ML_HEADS),
                        ("g_sb", D_MODEL), ("g_ml", D_MODEL)):
        cols[name] = w_in[:, o:o + width].astype(BF16)
        o += width
    lane_pad = ((0, 0), (0, LANES - ML_HEADS))
    cols["ml_i"] = jnp.pad(cols["ml_i"], lane_pad)
    cols["ml_f"] = jnp.pad(cols["ml_f"], lane_pad)
    bias_i = jnp.pad(b_igate.astype(F32).reshape(1, ML_HEADS), lane_pad)
    bias_f = jnp.pad(b_fgate.astype(F32).reshape(1, ML_HEADS), lane_pad)
    return cols, bias_i, bias_f


def _project(xb, w, out_dtypes, name, *, scale=None, bias=None):
    tm = _row_tile(xb.shape[0])
    tn = min(512, w.shape[1])

    def epilogue(accs, extras):
        y = accs[0]
        if scale is not None:
            y = y * scale
        if bias is not None:
            y = y + extras[0]
        return [y] * len(out_dtypes)

    extras = [] if bias is None else [bias]
    return _fused_matmul([(xb, w)], extras, epilogue, out_dtypes, tm=tm, tn=tn, name=name)


def _branch_inputs(xb, wc, bias_i, bias_f):
    (q_sb,) = _project(xb, wc["sb_q"], [BF16], "proj_sb_q")
    k_f, k_b = _project(xb, wc["sb_k"], [F32, BF16], "proj_sb_k")
    v_f, v_b = _project(xb, wc["sb_v"], [F32, BF16], "proj_sb_v")
    (q_ml,) = _project(xb, wc["ml_q"], [BF16], "proj_ml_q")
    (k_ml,) = _project(xb, wc["ml_k"], [BF16], "proj_ml_k", scale=ML_QK_DIM ** -0.5)
    (v_ml,) = _project(xb, wc["ml_v"], [BF16], "proj_ml_v")
    (o_ml,) = _project(xb, wc["ml_o"], [F32], "proj_ml_o")
    (gi,) = _project(xb, wc["ml_i"], [F32], "proj_ml_i", bias=bias_i)
    (gf,) = _project(xb, wc["ml_f"], [F32], "proj_ml_f", bias=bias_f)
    (g_sb,) = _project(xb, wc["g_sb"], [F32], "proj_g_sb")
    (g_ml,) = _project(xb, wc["g_ml"], [F32], "proj_g_ml")
    return q_sb, (k_f, k_b), (v_f, v_b), (q_ml, k_ml, v_ml), o_ml, (gi, gf), (g_sb, g_ml)


def _merge_and_ffn(x, o_sb, h_ml, g_sb, g_ml, p, w):
    m = x.shape[0]
    tm = _row_tile(m)

    def merge_epi(accs, extras):
        return [jax.nn.sigmoid(extras[0]) * accs[0] + jax.nn.sigmoid(extras[1]) * accs[1]]

    (mix,) = _fused_matmul([(o_sb, w["sb_proj"]), (h_ml, w["ml_proj"])], [g_sb, g_ml], merge_epi, [BF16],
                           tm=tm, tn=512, name="merge")

    def resid_epi(accs, extras):
        return [DEEPNORM_ALPHA * extras[0] + accs[0]]

    (r1,) = _fused_matmul([(mix, w["out"])], [x], resid_epi, [F32], tm=tm, tn=512, name="out_proj")
    x1, x1b = _layer_norm(r1, w["ln1_g"], w["ln1_b"], name="ln1")

    def swiglu_epi(accs, extras):
        g = accs[0]
        return [g * jax.nn.sigmoid(g) * accs[1]]

    (act,) = _fused_matmul([(x1b, w["ffn_gate"]), (x1b, w["ffn_up"])], [], swiglu_epi, [BF16],
                           tm=tm, tn=256, name="ffn_up")
    (r2,) = _fused_matmul([(act, w["ffn_down"])], [x1], resid_epi, [F32],
                          tm=min(tm, 512), tn=256, name="ffn_down")
    x2, x2b = _layer_norm(r2, w["ln2_g"], w["ln2_b"], name="ln2")

    def ple_epi(accs, extras):
        return [extras[0] + accs[1] * jax.nn.sigmoid(accs[0])]

    (y,) = _fused_matmul([(x2b, w["ple_gate"]), (p.astype(BF16), w["ple"])], [x2], ple_epi, [F32],
                         tm=tm, tn=512, name="ple")
    return y


def kernel(x_prompt, x_sample, cache_k, cache_v, state_C, state_n, state_m, page_table, p_prompt, p_sample,
           w_in, b_igate, b_fgate, b_sb, ml_norm_g, w_sb_proj, w_ml_proj, w_out, ln1_g, ln1_b,
           w_ffn_gate, w_ffn_up, w_ffn_down, ln2_g, ln2_b, w_ple, w_ple_gate):
    assert w_in.shape[0] == DEPTH == 1
    batch, seq, _ = x_prompt.shape
    dec_batch, dec_seq, _ = x_sample.shape
    sdt = state_C.dtype
    l = 0

    wc, bias_i, bias_f = _split_w_in(w_in[l], b_igate[l], b_fgate[l])
    w = {
        "sb_proj": w_sb_proj[l].astype(BF16),
        "ml_proj": w_ml_proj[l].astype(BF16),
        "out": w_out[l].astype(BF16),
        "ffn_gate": w_ffn_gate[l].astype(BF16),
        "ffn_up": w_ffn_up[l].astype(BF16),
        "ffn_down": w_ffn_down[l].astype(BF16),
        "ple": w_ple[l].astype(BF16),
        "ple_gate": w_ple_gate[l].astype(BF16),
        "ln1_g": ln1_g[l], "ln1_b": ln1_b[l], "ln2_g": ln2_g[l], "ln2_b": ln2_b[l],
    }

    mp = batch * seq
    xp = x_prompt.reshape(mp, D_MODEL)
    q_sb, (k_f, k_b), (v_f, v_b), (q_ml, k_ml, v_ml), o_ml, (gi, gf), (g_sb, g_ml) = _branch_inputs(
        xp.astype(BF16), wc, bias_i, bias_f)
    o_sb = _sb_prompt(q_sb, k_b, v_b, b_sb[l], batch, seq)
    zeros_c = jnp.zeros((batch, ML_HEADS, ML_QK_DIM, ML_V_DIM), F32)
    zeros_n = jnp.zeros((batch, ML_HEADS, ML_QK_DIM), F32)
    zeros_m = jnp.zeros((batch, 1, LANES), F32)
    h_ml, c_p, n_p, m_p = _mlstm(q_ml, k_ml, v_ml, gi, gf, o_ml, ml_norm_g[l], zeros_c, zeros_n, zeros_m,
                                 batch, seq // ML_CHUNK, ML_CHUNK)
    y_prompt = _merge_and_ffn(xp, o_sb, h_ml, g_sb, g_ml, p_prompt[l].reshape(mp, -1), w)
    k_prompt = k_f.reshape(1, batch, seq, SB_HEADS, SB_HEAD_DIM)
    v_prompt = v_f.reshape(1, batch, seq, SB_HEADS, SB_HEAD_DIM)
    c_prompt = c_p.astype(sdt)[None]
    n_prompt = n_p.astype(sdt)[None]
    m_prompt = m_p[:, 0, :ML_HEADS].astype(sdt)[None]

    ms = dec_batch * dec_seq
    xs = x_sample.reshape(ms, D_MODEL)
    q_sb, (k_f, k_b), (v_f, v_b), (q_ml, k_ml, v_ml), o_ml, (gi, gf), (g_sb, g_ml) = _branch_inputs(
        xs.astype(BF16), wc, bias_i, bias_f)
    o_sb = _sb_sample(q_sb, k_b, v_b, cache_k.reshape(cache_k.shape[1:]), cache_v.reshape(cache_v.shape[1:]),
                      page_table, b_sb[l], dec_batch, dec_seq)

    def pad_chunk(a, value=0.0):
        a3 = a.reshape(dec_batch, dec_seq, a.shape[-1])
        a3 = jnp.pad(a3, ((0, 0), (0, ML_CHUNK - dec_seq), (0, 0)), constant_values=value)
        return a3.reshape(dec_batch * ML_CHUNK, a.shape[-1])

    m0 = jnp.pad(state_m[l].astype(F32), ((0, 0), (0, LANES - ML_HEADS))).reshape(dec_batch, 1, LANES)
    h_pad, c_s, n_s, m_s = _mlstm(
        pad_chunk(q_ml), pad_chunk(k_ml), pad_chunk(v_ml), pad_chunk(gi, PAD_GATE_NEG), pad_chunk(gf, PAD_GATE_POS),
        pad_chunk(o_ml), ml_norm_g[l], state_C[l].astype(F32), state_n[l].astype(F32), m0,
        dec_batch, 1, ML_CHUNK)
    h_ml = h_pad.reshape(dec_batch, ML_CHUNK, ML_V_WIDTH)[:, :dec_seq].reshape(ms, ML_V_WIDTH)
    y_sample = _merge_and_ffn(xs, o_sb.astype(BF16), h_ml, g_sb, g_ml, p_sample[l].reshape(ms, -1), w)
    k_sample = k_f.reshape(1, dec_batch, dec_seq, SB_HEADS, SB_HEAD_DIM)
    v_sample = v_f.reshape(1, dec_batch, dec_seq, SB_HEADS, SB_HEAD_DIM)
    c_sample = c_s.astype(sdt)[None]
    n_sample = n_s.astype(sdt)[None]
    m_sample = m_s[:, 0, :ML_HEADS].astype(sdt)[None]

    return (y_prompt.reshape(batch, seq, D_MODEL), y_sample.reshape(dec_batch, dec_seq, D_MODEL),
            k_prompt, v_prompt, c_prompt, n_prompt, m_prompt,
            k_sample, v_sample, c_sample, n_sample, m_sample)
```
